```python
import math, functools
import jax, jax.numpy as jnp
from jax import lax
import numpy as np

D_MODEL = 1024
BATCH = 4
SEQ = 4096
DEPTH = 1
DEC_BATCH = 128
DEC_SEQ = 1
PAST_LEN = 2048
PAGE_SIZE = 128

D_MIX = D_MODEL
D_RNN = D_MIX // 2
RNN_BLOCKS = 8
RNN_BLOCK_W = D_RNN // RNN_BLOCKS
CONV_W = 4
RGLRU_C = 8.0
HEAD_DIM = 64
D_ATT = D_MIX - D_RNN
N_HEADS = D_ATT // HEAD_DIM
ROT_DIM = HEAD_DIM // 4
ROPE_THETA = 500000.0
DILATED_BRANCHES = ((128, 1), (512, 4), (2048, 16))
WINDOW_MAX = max(w for w, _ in DILATED_BRANCHES)
ATTN_SCALE = HEAD_DIM ** -0.5
D_IN = 2 * D_RNN + 3 * D_ATT
SPLITS = (D_RNN, 2 * D_RNN, 2 * D_RNN + D_ATT, 2 * D_RNN + 2 * D_ATT)
N_EXPERTS = 64
N_GROUPS = 8
TOPK_GROUPS = 4
TOP_K = 8
D_EXPERT = 256
D_SHARED = 256
ROUTED_SCALE = 2.5
DISPATCH_BLOCK = 128
PLE_DIM = 256
EPS = 1e-6

kernel_name = 'hymba_rglru_dilated_moe_step'


def rms_norm(x, g):
    xf = x.astype(jnp.float32)
    y = xf * lax.rsqrt(jnp.mean(xf * xf, axis=-1, keepdims=True) + EPS) * g.astype(jnp.float32)
    return y.astype(x.dtype)


def partial_rope(x, pos):
    half = ROT_DIM // 2
    inv_freq = ROPE_THETA ** (-jnp.arange(half, dtype=jnp.float32) / half)
    ang = pos.astype(jnp.float32)[:, None] * inv_freq[None, :]
    cos = jnp.cos(ang)[None, :, None, :]
    sin = jnp.sin(ang)[None, :, None, :]
    xf = x.astype(jnp.float32)
    x1, x2 = xf[..., :half], xf[..., half:ROT_DIM]
    out = jnp.concatenate([x1 * cos - x2 * sin, x2 * cos + x1 * sin, xf[..., ROT_DIM:]], axis=-1)
    return out.astype(x.dtype)


def causal_conv(x, buf, w, b):
    T = x.shape[1]
    xp = jnp.concatenate([buf.astype(x.dtype), x], axis=1)
    y = b.astype(x.dtype)
    for j in range(CONV_W):
        y = y + xp[:, j:j + T] * w[j]
    return y, xp[:, -(CONV_W - 1):]


def rglru(xc, h0, w_a, b_a, w_i, b_i, lam):
    B, T, _ = xc.shape
    xf = xc.astype(jnp.float32)
    xb = xf.reshape(B, T, RNN_BLOCKS, RNN_BLOCK_W)
    gate_r = jax.nn.sigmoid(jnp.einsum('btnc,ncd->btnd', xb, w_a.astype(jnp.float32)).reshape(B, T, D_RNN)
                            + b_a.astype(jnp.float32))
    gate_i = jax.nn.sigmoid(jnp.einsum('btnc,ncd->btnd', xb, w_i.astype(jnp.float32)).reshape(B, T, D_RNN)
                            + b_i.astype(jnp.float32))
    log_a = -RGLRU_C * jax.nn.softplus(-lam.astype(jnp.float32)) * gate_r
    a = jnp.exp(log_a)
    bx = jnp.sqrt(-jnp.expm1(2.0 * log_a)) * gate_i * xf

    def step(h, ab):
        a_t, b_t = ab
        h = a_t * h + b_t
        return h, h

    h_last, hs = lax.scan(step, h0.astype(jnp.float32), (jnp.swapaxes(a, 0, 1), jnp.swapaxes(bx, 0, 1)))
    return jnp.swapaxes(hs, 0, 1), h_last


def dilated_branch_prompt(q, k, v, window, dil):
    B, S, H, Dh = q.shape
    nk = window // dil
    L = S // dil
    nb = -(-L // nk)
    Lp = nb * nk

    def strided(t):
        t = t.reshape(B, L, dil, H, Dh).transpose(0, 2, 1, 3, 4)
        t = jnp.pad(t, ((0, 0), (0, 0), (0, Lp - L), (0, 0), (0, 0)))
        return t.reshape(B, dil, nb, nk, H, Dh)

    def with_prev(t):
        prev = jnp.pad(t, ((0, 0), (0, 0), (1, 0), (0, 0), (0, 0), (0, 0)))[:, :, :-1]
        return jnp.concatenate([prev, t], axis=3)

    qb = strided(q)
    kk = with_prev(strided(k))
    vv = with_prev(strided(v))
    s = jnp.einsum('bgnqhd,bgnkhd->bgnhqk', qb, kk, preferred_element_type=jnp.float32) * ATTN_SCALE
    qi = jnp.arange(nk)[:, None]
    ki = jnp.arange(2 * nk)[None, :]
    dist = qi + nk - ki
    blk = jnp.arange(nb)[:, None, None]
    valid = (dist >= 0) & (dist <= nk) & (blk * nk - nk + ki >= 0)
    s = jnp.where(valid[None, None, :, None], s, -jnp.inf)
    m = jnp.max(s, axis=-1, keepdims=True)
    e = jnp.exp(s - m)
    z = jnp.sum(e, axis=-1, keepdims=True)
    o = jnp.einsum('bgnhqk,bgnkhd->bgnqhd', e, vv.astype(jnp.float32)) / jnp.swapaxes(z, 3, 4)
    lse = jnp.swapaxes((m + jnp.log(z))[..., 0], 3, 4)
    o = o.reshape(B, dil, Lp, H, Dh)[:, :, :L].transpose(0, 2, 1, 3, 4).reshape(B, S, H, Dh)
    lse = lse.reshape(B, dil, Lp, H)[:, :, :L].transpose(0, 2, 1, 3).reshape(B, S, H)
    return o, lse


def dilated_branch_sample(q, kc, vc, window, dil, n_past):
    T = q.shape[1]
    nk = window // dil
    idx = n_past + jnp.arange(T)[:, None] - dil * jnp.arange(nk + 1)[None, :]
    valid = idx >= 0
    idxc = jnp.maximum(idx, 0)
    kg = kc[:, idxc]
    vg = vc[:, idxc]
    s = jnp.einsum('bthd,btjhd->bthj', q, kg, preferred_element_type=jnp.float32) * ATTN_SCALE
    s = jnp.where(valid[None, :, None, :], s, -jnp.inf)
    m = jnp.max(s, axis=-1, keepdims=True)
    e = jnp.exp(s - m)
    z = jnp.sum(e, axis=-1, keepdims=True)
    o = jnp.einsum('bthj,btjhd->bthd', e, vg.astype(jnp.float32)) / z
    return o, (m + jnp.log(z))[..., 0]


def merge_branches(outs, lses):
    w = jax.nn.softmax(jnp.stack(lses, axis=0), axis=0)
    return jnp.sum(w[..., None] * jnp.stack(outs, axis=0), axis=0)


def dilated_attention_prompt(q, k, v):
    res = [dilated_branch_prompt(q, k, v, w, d) for (w, d) in DILATED_BRANCHES]
    o = merge_branches([r[0] for r in res], [r[1] for r in res])
    keep = min(WINDOW_MAX, k.shape[1])
    return o.astype(q.dtype), k[:, -keep:], v[:, -keep:]


def dilated_attention_sample(q, k, v, k_buf, v_buf):
    n_past = k_buf.shape[1]
    kc = jnp.concatenate([k_buf.astype(k.dtype), k], axis=1)
    vc = jnp.concatenate([v_buf.astype(v.dtype), v], axis=1)
    res = [dilated_branch_sample(q, kc, vc, w, d, n_past) for (w, d) in DILATED_BRANCHES]
    o = merge_branches([r[0] for r in res], [r[1] for r in res])
    return o.astype(q.dtype), k, v


def route(u, w_router, b_router):
    N = u.shape[0]
    scores = jax.nn.sigmoid(u.astype(jnp.float32) @ w_router.astype(jnp.float32))
    biased = scores + b_router.astype(jnp.float32)
    grp = biased.reshape(N, N_GROUPS, N_EXPERTS // N_GROUPS)
    grp_score = jnp.sum(lax.top_k(grp, 2)[0], axis=-1)
    _, gidx = lax.top_k(grp_score, TOPK_GROUPS)
    gmask = jnp.sum(jax.nn.one_hot(gidx, N_GROUPS, dtype=jnp.float32), axis=1) > 0
    emask = jnp.repeat(gmask, N_EXPERTS // N_GROUPS, axis=-1)
    _, idx = lax.top_k(jnp.where(emask, biased, -jnp.inf), TOP_K)
    w = jnp.take_along_axis(scores, idx, axis=-1)
    w = w / jnp.sum(w, axis=-1, keepdims=True) * ROUTED_SCALE
    return idx, w


def moe_routed(x2, idx, wts, we_gate, we_up, we_down):
    N, D = x2.shape
    M = N * TOP_K
    flat_e = idx.reshape(M)
    flat_tok = jnp.repeat(jnp.arange(N, dtype=jnp.int32), TOP_K)
    flat_w = wts.reshape(M)
    order = jnp.argsort(flat_e)
    e_s, tok_s, w_s = flat_e[order], flat_tok[order], flat_w[order]
    counts = jnp.bincount(flat_e, length=N_EXPERTS)
    padded = (counts + DISPATCH_BLOCK - 1) // DISPATCH_BLOCK * DISPATCH_BLOCK
    pad_end = jnp.cumsum(padded)
    pad_start = pad_end - padded
    start = jnp.cumsum(counts) - counts
    dest = pad_start[e_s] + jnp.arange(M) - start[e_s]
    n_blocks = -(-(M + N_EXPERTS * (DISPATCH_BLOCK - 1)) // DISPATCH_BLOCK)
    cap = n_blocks * DISPATCH_BLOCK
    slot_tok = jnp.zeros((cap,), jnp.int32).at[dest].set(tok_s)
    slot_w = jnp.zeros((cap,), jnp.float32).at[dest].set(w_s)
    block_start = jnp.arange(n_blocks) * DISPATCH_BLOCK
    block_e = jnp.minimum(jnp.searchsorted(pad_end, block_start, side='right'), N_EXPERTS - 1)

    def body(acc, blk):
        tok, wv, e = blk
        xb = x2[tok]
        hid = jax.nn.silu(xb @ we_gate[e]) * (xb @ we_up[e])
        yb = (hid @ we_down[e]).astype(jnp.float32) * wv[:, None]
        return acc.at[tok].add(yb), None

    acc, _ = lax.scan(body, jnp.zeros((N, D), jnp.float32),
                      (slot_tok.reshape(n_blocks, DISPATCH_BLOCK), slot_w.reshape(n_blocks, DISPATCH_BLOCK), block_e))
    return acc


def hybrid_layer(h, pl, pos, conv_buf, rnn_h0, attend,
                 g_mix, w_in, conv_w, conv_b, rg_wa, rg_ba, rg_wi, rg_bi, rg_lambda,
                 g_q, g_k, g_out_rnn, g_out_att, w_out, g_ffn, w_router, b_router,
                 we_gate, we_up, we_down, ws_gate, ws_up, ws_down, g_ple, w_ple_gate, w_ple_proj):
    B, T, _ = h.shape
    u = rms_norm(h, g_mix)
    z = u @ w_in
    xr, yr, q, k, v = jnp.split(z, SPLITS, axis=-1)
    xc, conv_new = causal_conv(xr, conv_buf, conv_w, conv_b)
    hs, h_last = rglru(xc, rnn_h0, rg_wa, rg_ba, rg_wi, rg_bi, rg_lambda)
    out_rnn = jax.nn.gelu(yr) * hs.astype(yr.dtype)
    q = partial_rope(rms_norm(q.reshape(B, T, N_HEADS, HEAD_DIM), g_q), pos)
    k = partial_rope(rms_norm(k.reshape(B, T, N_HEADS, HEAD_DIM), g_k), pos)
    v = v.reshape(B, T, N_HEADS, HEAD_DIM)
    o_att, k_rows, v_rows = attend(q, k, v)
    mix = jnp.concatenate([rms_norm(out_rnn, g_out_rnn), rms_norm(o_att.reshape(B, T, D_ATT), g_out_att)], axis=-1)
    h = h + mix @ w_out
    u2 = rms_norm(h, g_ffn).reshape(B * T, D_MODEL)
    shared = (jax.nn.silu(u2 @ ws_gate) * (u2 @ ws_up)) @ ws_down
    idx, wts = route(u2, w_router, b_router)
    routed = moe_routed(u2, idx, wts, we_gate, we_up, we_down)
    h = h + (shared.astype(jnp.float32) + routed).astype(h.dtype).reshape(B, T, D_MODEL)
    u3 = rms_norm(h, g_ple)
    h = h + jax.nn.sigmoid(u3 @ w_ple_gate) * (pl @ w_ple_proj)
    return h, k_rows, v_rows, h_last, conv_new


def setup_inputs(seed: int = 0) -> dict:
    key = jax.random.key(seed)
    ks = iter(jax.random.split(key, 64))

    def nrm(shape, scale):
        return scale * jax.random.normal(next(ks), shape, jnp.float32)

    def gain(shape):
        return 1.0 + nrm(shape, 0.05)

    wb = min(WINDOW_MAX, PAST_LEN)
    a_pow = jax.random.uniform(next(ks), (DEPTH, D_RNN), jnp.float32, 0.9, 0.999)
    sig = a_pow ** (1.0 / RGLRU_C)
    rg_lambda = jnp.log(sig) - jnp.log1p(-sig)
    return {
        'x_prompt': nrm((BATCH, SEQ, D_MODEL), 1.0),
        'x_sample': nrm((DEC_BATCH, DEC_SEQ, D_MODEL), 1.0),
        'p_prompt': nrm((DEPTH, BATCH, SEQ, PLE_DIM), 1.0),
        'p_sample': nrm((DEPTH, DEC_BATCH, DEC_SEQ, PLE_DIM), 1.0),
        'cache_win_k': nrm((DEPTH, DEC_BATCH, wb, N_HEADS, HEAD_DIM), 1.0),
        'cache_win_v': nrm((DEPTH, DEC_BATCH, wb, N_HEADS, HEAD_DIM), 1.0),
        'state_rnn_h': nrm((DEPTH, DEC_BATCH, D_RNN), 0.5),
        'state_conv': nrm((DEPTH, DEC_BATCH, CONV_W - 1, D_RNN), 1.0),
        'g_mix': gain((DEPTH, D_MODEL)),
        'w_in': nrm((DEPTH, D_MODEL, D_IN), D_MODEL ** -0.5),
        'conv_w': nrm((DEPTH, CONV_W, D_RNN), CONV_W ** -0.5),
        'conv_b': nrm((DEPTH, D_RNN), 0.01),
        'rg_wa': nrm((DEPTH, RNN_BLOCKS, RNN_BLOCK_W, RNN_BLOCK_W), RNN_BLOCK_W ** -0.5),
        'rg_ba': nrm((DEPTH, D_RNN), 0.1),
        'rg_wi': nrm((DEPTH, RNN_BLOCKS, RNN_BLOCK_W, RNN_BLOCK_W), RNN_BLOCK_W ** -0.5),
        'rg_bi': nrm((DEPTH, D_RNN), 0.1),
        'rg_lambda': rg_lambda,
        'g_q': gain((DEPTH, HEAD_DIM)),
        'g_k': gain((DEPTH, HEAD_DIM)),
        'g_out_rnn': gain((DEPTH, D_RNN)),
        'g_out_att': gain((DEPTH, D_ATT)),
        'w_out': nrm((DEPTH, D_MIX, D_MODEL), D_MIX ** -0.5),
        'g_ffn': gain((DEPTH, D_MODEL)),
        'w_router': nrm((DEPTH, D_MODEL, N_EXPERTS), D_MODEL ** -0.5),
        'b_router': nrm((DEPTH, N_EXPERTS), 0.01),
        'we_gate': nrm((DEPTH, N_EXPERTS, D_MODEL, D_EXPERT), D_MODEL ** -0.5),
        'we_up': nrm((DEPTH, N_EXPERTS, D_MODEL, D_EXPERT), D_MODEL ** -0.5),
        'we_down': nrm((DEPTH, N_EXPERTS, D_EXPERT, D_MODEL), D_EXPERT ** -0.5),
        'ws_gate': nrm((DEPTH, D_MODEL, D_SHARED), D_MODEL ** -0.5),
        'ws_up': nrm((DEPTH, D_MODEL, D_SHARED), D_MODEL ** -0.5),
        'ws_down': nrm((DEPTH, D_SHARED, D_MODEL), D_SHARED ** -0.5),
        'g_ple': gain((DEPTH, D_MODEL)),
        'w_ple_gate': nrm((DEPTH, D_MODEL, D_MODEL), D_MODEL ** -0.5),
        'w_ple_proj': nrm((DEPTH, PLE_DIM, D_MODEL), PLE_DIM ** -0.5),
    }


def reference(x_prompt, x_sample, p_prompt, p_sample, cache_win_k, cache_win_v, state_rnn_h, state_conv,
              g_mix, w_in, conv_w, conv_b, rg_wa, rg_ba, rg_wi, rg_bi, rg_lambda, g_q, g_k,
              g_out_rnn, g_out_att, w_out, g_ffn, w_router, b_router, we_gate, we_up, we_down,
              ws_gate, ws_up, ws_down, g_ple, w_ple_gate, w_ple_proj):
    h_p, h_s = x_prompt, x_sample
    bp, tp = h_p.shape[0], h_p.shape[1]
    ts = h_s.shape[1]
    pos_p = jnp.arange(tp, dtype=jnp.int32)
    pos_s = PAST_LEN + jnp.arange(ts, dtype=jnp.int32)
    kp_l, vp_l, ks_l, vs_l, rp_l, rs_l, cp_l, cs_l = [], [], [], [], [], [], [], []
    for i in range(DEPTH):
        lw = (g_mix[i], w_in[i], conv_w[i], conv_b[i], rg_wa[i], rg_ba[i], rg_wi[i], rg_bi[i], rg_lambda[i],
              g_q[i], g_k[i], g_out_rnn[i], g_out_att[i], w_out[i], g_ffn[i], w_router[i], b_router[i],
              we_gate[i], we_up[i], we_down[i], ws_gate[i], ws_up[i], ws_down[i], g_ple[i], w_ple_gate[i], w_ple_proj[i])
        h_p, kp, vp, rp, cp = hybrid_layer(
            h_p, p_prompt[i], pos_p,
            jnp.zeros((bp, CONV_W - 1, D_RNN), h_p.dtype), jnp.zeros((bp, D_RNN), jnp.float32),
            dilated_attention_prompt, *lw)
        attend_s = functools.partial(dilated_attention_sample, k_buf=cache_win_k[i], v_buf=cache_win_v[i])
        h_s, ks_, vs_, rs, cs = hybrid_layer(
            h_s, p_sample[i], pos_s, state_conv[i], state_rnn_h[i], attend_s, *lw)
        kp_l.append(kp); vp_l.append(vp); ks_l.append(ks_); vs_l.append(vs_)
        rp_l.append(rp); rs_l.append(rs); cp_l.append(cp); cs_l.append(cs)
    return (h_p, h_s,
            jnp.stack(kp_l), jnp.stack(vp_l), jnp.stack(ks_l), jnp.stack(vs_l),
            jnp.stack(rp_l), jnp.stack(rs_l), jnp.stack(cp_l), jnp.stack(cs_l))
```

```python
import functools

import jax
import jax.numpy as jnp
from jax import lax
from jax.experimental import pallas as pl
from jax.experimental.pallas import tpu as pltpu

F32 = jnp.float32
BF16 = jnp.bfloat16
I32 = jnp.int32
U32 = jnp.uint32

EPS = 1e-6
LANES = 128
SUBLANES = 8
VMEM_LIMIT = 56 * 1024 * 1024

D_MODEL = 1024
D_RNN = 512
RNN_BLOCKS = 8
CONV_W = 4
RGLRU_C = 8.0
HEAD_DIM = 64
D_ATT = 512
N_HEADS = 8
ROT_DIM = 16
ROPE_THETA = 500000.0
BRANCHES = ((128, 1), (512, 4), (2048, 16))
BAND = 128
ATTN_SCALE = HEAD_DIM ** -0.5
D_IN = 2 * D_RNN + 3 * D_ATT
N_EXPERTS = 64
N_GROUPS = 8
GROUP_SIZE = N_EXPERTS // N_GROUPS
TOPK_GROUPS = 4
TOP_K = 8
D_EXPERT = 256
D_SHARED = 256
ROUTED_SCALE = 2.5
PLE_DIM = 256
PAST_LEN = 2048

MOE_ROWS = 128
MOE_STRIDE = MOE_ROWS + 1
NEG = -1e30


def _cparams(n_axes):
    return pltpu.CompilerParams(dimension_semantics=("arbitrary",) * n_axes,
                                vmem_limit_bytes=VMEM_LIMIT)


def _rms(x, g):
    ms = jnp.mean(x * x, axis=-1, keepdims=True)
    return x * lax.rsqrt(ms + EPS) * g


def _sigmoid(x):
    return 1.0 / (1.0 + jnp.exp(-x))


def _const_spec(shape):
    nd = len(shape)
    return pl.BlockSpec(shape, lambda *_: (0,) * nd)


def _head_norm_rope(x, g, ones_bd, cos, sa, sb):
    xsq = x * x
    hi = xsq.astype(BF16)
    lo = (xsq - hi.astype(F32)).astype(BF16)
    ms = (jnp.dot(hi, ones_bd, preferred_element_type=F32)
          + jnp.dot(lo, ones_bd, preferred_element_type=F32)) * (1.0 / HEAD_DIM)
    y = x * lax.rsqrt(ms + EPS) * g
    outs = []
    for c in range(D_ATT // LANES):
        yc = y[:, c * LANES:(c + 1) * LANES]
        dn = pltpu.roll(yc, LANES - ROT_DIM // 2, axis=1)
        up = pltpu.roll(yc, ROT_DIM // 2, axis=1)
        outs.append(yc * cos + dn * sa + up * sb)
    return jnp.concatenate(outs, axis=-1)


def _inproj_kernel(x_ref, g_ref, w_ref, gq_ref, gk_ref, ones_ref, cos_ref, sa_ref, sb_ref,
                   xr_ref, yr_ref, q_ref, k_ref, v_ref):
    u = _rms(x_ref[...], g_ref[...]).astype(BF16)
    z = jnp.dot(u, w_ref[...], preferred_element_type=F32)
    xr_ref[...] = z[:, :D_RNN]
    yr_ref[...] = z[:, D_RNN:2 * D_RNN]
    v_ref[...] = z[:, 2 * D_RNN + 2 * D_ATT:]
    cos, sa, sb = cos_ref[...], sa_ref[...], sb_ref[...]
    ones_bd = ones_ref[...]
    q_ref[...] = _head_norm_rope(z[:, 2 * D_RNN:2 * D_RNN + D_ATT], gq_ref[...], ones_bd, cos, sa, sb)
    k_ref[...] = _head_norm_rope(z[:, 2 * D_RNN + D_ATT:2 * D_RNN + 2 * D_ATT], gk_ref[...],
                                 ones_bd, cos, sa, sb)


def _rope_tables(pos):
    half = ROT_DIM // 2
    inv_freq = ROPE_THETA ** (-jnp.arange(half, dtype=F32) / half)
    ang = pos.astype(F32)[:, None] * inv_freq[None, :]
    cos, sin = jnp.cos(ang), jnp.sin(ang)
    t = pos.shape[0]
    pad = jnp.zeros((t, HEAD_DIM - ROT_DIM), F32)
    cos_h = jnp.concatenate([cos, cos, pad + 1.0], axis=1)
    sa_h = jnp.concatenate([-sin, jnp.zeros_like(sin), pad], axis=1)
    sb_h = jnp.concatenate([jnp.zeros_like(sin), sin, pad], axis=1)
    two = lambda a: jnp.concatenate([a, a], axis=1)
    return two(cos_h), two(sa_h), two(sb_h)


def _in_proj(x2d, g_mix, w_in_bf, gq_t, gk_t, ones_bd, tables, tm, table_tiles):
    n = x2d.shape[0]
    cos, sa, sb = tables
    row = lambda i: (i, 0)
    tab = lambda i: (i % table_tiles, 0)
    out = jax.ShapeDtypeStruct((n, D_RNN), F32)
    return pl.pallas_call(
        _inproj_kernel,
        grid=(n // tm,),
        in_specs=[pl.BlockSpec((tm, D_MODEL), row), _const_spec((1, D_MODEL)),
                  _const_spec((D_MODEL, D_IN)), _const_spec((1, D_ATT)), _const_spec((1, D_ATT)),
                  _const_spec((D_ATT, D_ATT)),
                  pl.BlockSpec((tm, LANES), tab), pl.BlockSpec((tm, LANES), tab),
                  pl.BlockSpec((tm, LANES), tab)],
        out_specs=[pl.BlockSpec((tm, D_RNN), row)] * 5,
        out_shape=[out] * 5,
        compiler_params=_cparams(1),
        name="in_proj",
    )(x2d, g_mix, w_in_bf, gq_t, gk_t, ones_bd, cos, sa, sb)


def _rglru_gates(xc, wa_ref, ba_ref, wi_ref, bi_ref, clam_ref):
    xb = xc.astype(BF16)
    gate_r = _sigmoid(jnp.dot(xb, wa_ref[...], preferred_element_type=F32) + ba_ref[...])
    gate_i = _sigmoid(jnp.dot(xb, wi_ref[...], preferred_element_type=F32) + bi_ref[...])
    log_a = clam_ref[...] * gate_r
    a = jnp.exp(log_a)
    one_minus_a2 = -jnp.tanh(log_a) * (a * a + 1.0)
    bx = jnp.sqrt(one_minus_a2) * gate_i * xc
    return a, bx


def _scan_rows(a, b, tt):
    rows = lax.broadcasted_iota(I32, a.shape, 0)
    s = 1
    while s < tt:
        keep = rows >= s
        a_sh = jnp.where(keep, pltpu.roll(a, s, axis=0), 1.0)
        b_sh = jnp.where(keep, pltpu.roll(b, s, axis=0), 0.0)
        b = a * b_sh + b
        a = a * a_sh
        s *= 2
    return a, b


def _rglru_prompt_kernel(xr_ref, yr_ref, cw_ref, cb_ref, wa_ref, ba_ref, wi_ref, bi_ref, clam_ref,
                         g_ref, mix_ref, hlast_ref, tail_ref, xpad_ref, h_ref, *, tt):
    t = pl.program_id(1)

    @pl.when(t == 0)
    def _():
        xpad_ref[pl.ds(0, SUBLANES), :] = jnp.zeros((SUBLANES, D_RNN), F32)
        h_ref[...] = jnp.zeros_like(h_ref)

    x = xr_ref[...]
    xpad_ref[pl.ds(SUBLANES, tt), :] = x
    cw = cw_ref[...]
    xc = cb_ref[...]
    for j in range(CONV_W):
        xc = xc + xpad_ref[pl.ds(SUBLANES - (CONV_W - 1) + j, tt), :] * cw[j:j + 1, :]
    tail = x[tt - SUBLANES:, :]
    xpad_ref[pl.ds(0, SUBLANES), :] = tail
    tail_ref[...] = tail

    a, bx = _rglru_gates(xc, wa_ref, ba_ref, wi_ref, bi_ref, clam_ref)
    a_cum, b_cum = _scan_rows(a, bx, tt)
    hs = a_cum * h_ref[...] + b_cum
    h_last = hs[tt - 1:tt, :]
    h_ref[...] = h_last
    hlast_ref[...] = h_last
    out = jax.nn.gelu(yr_ref[...]) * hs
    mix_ref[...] = _rms(out, g_ref[...]).astype(BF16)


def _rglru_prompt(xr, yr, rg, tt):
    b, t, _ = xr.shape
    seq = lambda i, j: (i, j, 0)
    per_b = lambda i, j: (i, 0, 0)
    vec = _const_spec((1, D_RNN))
    mat = _const_spec((D_RNN, D_RNN))
    return pl.pallas_call(
        functools.partial(_rglru_prompt_kernel, tt=tt),
        grid=(b, t // tt),
        in_specs=[pl.BlockSpec((None, tt, D_RNN), seq), pl.BlockSpec((None, tt, D_RNN), seq),
                  _const_spec((CONV_W, D_RNN)), vec, mat, vec, mat, vec, vec, vec],
        out_specs=[pl.BlockSpec((None, tt, D_RNN), seq),
                   pl.BlockSpec((None, 1, D_RNN), per_b),
                   pl.BlockSpec((None, SUBLANES, D_RNN), per_b)],
        out_shape=[jax.ShapeDtypeStruct((b, t, D_RNN), BF16),
                   jax.ShapeDtypeStruct((b, 1, D_RNN), F32),
                   jax.ShapeDtypeStruct((b, SUBLANES, D_RNN), F32)],
        scratch_shapes=[pltpu.VMEM((tt + SUBLANES, D_RNN), F32), pltpu.VMEM((1, D_RNN), F32)],
        compiler_params=_cparams(2),
        name="rglru_prompt",
    )(xr, yr, rg["conv_w"], rg["conv_b"], rg["wa"], rg["ba"], rg["wi"], rg["bi"], rg["clam"],
      rg["g_out"])


def _rglru_sample_kernel(xr_ref, yr_ref, buf_ref, h0_ref, cw_ref, cb_ref, wa_ref, ba_ref, wi_ref,
                         bi_ref, clam_ref, g_ref, mix_ref, h_ref, conv_ref):
    x = xr_ref[...]
    cw = cw_ref[...]
    xc = cb_ref[...]
    for j in range(CONV_W - 1):
        xc = xc + buf_ref[j] * cw[j:j + 1, :]
        if j > 0:
            conv_ref[j - 1] = buf_ref[j]
    xc = xc + x * cw[CONV_W - 1:CONV_W, :]
    conv_ref[CONV_W - 2] = x
    a, bx = _rglru_gates(xc, wa_ref, ba_ref, wi_ref, bi_ref, clam_ref)
    h = a * h0_ref[...] + bx
    h_ref[...] = h
    mix_ref[...] = _rms(jax.nn.gelu(yr_ref[...]) * h, g_ref[...]).astype(BF16)


def _rglru_sample(xr, yr, buf, h0, rg):
    n = xr.shape[0]
    full2 = _const_spec((n, D_RNN))
    full3 = _const_spec((CONV_W - 1, n, D_RNN))
    vec = _const_spec((1, D_RNN))
    mat = _const_spec((D_RNN, D_RNN))
    return pl.pallas_call(
        _rglru_sample_kernel,
        grid=(1,),
        in_specs=[full2, full2, full3, full2, _const_spec((CONV_W, D_RNN)), vec, mat, vec, mat, vec,
                  vec, vec],
        out_specs=[full2, full2, full3],
        out_shape=[jax.ShapeDtypeStruct((n, D_RNN), BF16), jax.ShapeDtypeStruct((n, D_RNN), F32),
                   jax.ShapeDtypeStruct((CONV_W - 1, n, D_RNN), F32)],
        compiler_params=_cparams(1),
        name="rglru_sample",
    )(xr, yr, buf, h0, rg["conv_w"], rg["conv_b"], rg["wa"], rg["ba"], rg["wi"], rg["bi"],
      rg["clam"], rg["g_out"])


def _attn_prompt_kernel(q_ref, k_ref, v_ref, o_ref, acc_ref, m_ref, l_ref, *, seq):
    lane = lax.broadcasted_iota(I32, (1, LANES), 1)
    head0 = lane < HEAD_DIM
    base = (lax.broadcasted_iota(I32, (BAND, 2 * BAND), 0)
            - lax.broadcasted_iota(I32, (BAND, 2 * BAND), 1))

    for bi, (_, dil) in enumerate(BRANCHES):
        n_tiles = seq // (dil * BAND)
        for r in range(dil):

            def tile(n, carry, bi=bi, dil=dil, r=r):
                kb = jnp.maximum(n - 1, 0)
                q_base = pl.multiple_of(n * (BAND * dil), BAND)
                k_base = pl.multiple_of(kb * (BAND * dil), BAND)
                if dil == 1:
                    q_rows = pl.ds(q_base, BAND)
                    q2 = q_ref[q_rows, :]
                    k2 = k_ref[pl.ds(k_base, 2 * BAND), :]
                    v2 = v_ref[pl.ds(k_base, 2 * BAND), :]
                else:
                    q_win = pl.ds(q_base, BAND * dil)
                    k_win = pl.ds(k_base, 2 * BAND * dil)
                    q_rows = pl.ds(r, BAND, stride=dil)
                    q2 = q_ref.at[q_win][q_rows, :]
                    k2 = k_ref.at[k_win][pl.ds(r, 2 * BAND, stride=dil), :]
                    v2 = v_ref.at[k_win][pl.ds(r, 2 * BAND, stride=dil), :]
                q2 = q2 * ATTN_SCALE
                k2 = k2.astype(BF16)
                v2 = v2.astype(BF16)
                dist = base + (n - kb) * BAND
                bias = jnp.where((dist >= 0) & (dist <= BAND), 0.0, NEG)
                res = []
                for h in range(2):
                    hm = head0 if h == 0 else jnp.logical_not(head0)
                    qh = jnp.where(hm, q2, 0.0).astype(BF16)
                    s = lax.dot_general(qh, k2, (((1,), (1,)), ((), ())),
                                        preferred_element_type=F32) + bias
                    m = jnp.max(s, axis=-1, keepdims=True)
                    p = jnp.exp(s - m)
                    l = jnp.sum(p, axis=-1, keepdims=True)
                    o = jnp.dot(p.astype(BF16), v2, preferred_element_type=F32)
                    res.append((o, m, l))
                o2 = jnp.where(head0, res[0][0], res[1][0])
                m2 = jnp.where(head0, res[0][1], res[1][1])
                l2 = jnp.where(head0, res[0][2], res[1][2])
                if bi == 0:
                    acc_ref[q_rows, :] = o2
                    m_ref[q_rows, :] = m2
                    l_ref[q_rows, :] = l2
                else:
                    acc_w, m_w, l_w = acc_ref.at[q_win], m_ref.at[q_win], l_ref.at[q_win]
                    m_old = m_w[q_rows, :]
                    m_new = jnp.maximum(m_old, m2)
                    e_old = jnp.exp(m_old - m_new)
                    e_new = jnp.exp(m2 - m_new)
                    acc_w[q_rows, :] = acc_w[q_rows, :] * e_old + o2 * e_new
                    l_w[q_rows, :] = l_w[q_rows, :] * e_old + l2 * e_new
                    m_w[q_rows, :] = m_new
                return carry

            lax.fori_loop(0, n_tiles, tile, 0)

    chunk = 512

    def finish(i, carry):
        rows = pl.ds(pl.multiple_of(i * chunk, chunk), chunk)
        o_ref[rows, :] = acc_ref[rows, :] / l_ref[rows, :]
        return carry

    lax.fori_loop(0, seq // chunk, finish, 0)


def _attn_prompt(q, k, v):
    b, s, _ = q.shape
    spec = pl.BlockSpec((None, s, LANES), lambda i, j: (i, 0, j))
    return pl.pallas_call(
        functools.partial(_attn_prompt_kernel, seq=s),
        grid=(b, D_ATT // LANES),
        in_specs=[spec, spec, spec],
        out_specs=spec,
        out_shape=jax.ShapeDtypeStruct((b, s, D_ATT), F32),
        scratch_shapes=[pltpu.VMEM((s, LANES), F32)] * 3,
        compiler_params=_cparams(2),
        name="attn_prompt",
    )(q, k, v)


def _attn_sample_kernel(q_ref, kn_ref, vn_ref, k1_ref, k4_ref, k16_ref, v1_ref, v4_ref, v16_ref,
                        o_ref):
    q = q_ref[...] * ATTN_SCALE
    s0 = jnp.sum(q * kn_ref[...], axis=-1, keepdims=True)
    ss = [jnp.sum(kr[...] * q[None], axis=-1, keepdims=True) for kr in (k1_ref, k4_ref, k16_ref)]
    m = s0
    for s in ss:
        m = jnp.maximum(m, jnp.max(s, axis=0))
    n_br = float(len(BRANCHES))
    p0 = jnp.exp(s0 - m) * n_br
    den = p0
    num = p0 * vn_ref[...]
    for s, vr in zip(ss, (v1_ref, v4_ref, v16_ref)):
        p = jnp.exp(s - m[None])
        den = den + jnp.sum(p, axis=0)
        num = num + jnp.sum(p * vr[...], axis=0)
    o_ref[...] = num / den


def _attn_sample(q, kn, vn, cache_k, cache_v):
    b = q.shape[0]
    row = pl.BlockSpec((None, N_HEADS, HEAD_DIM), lambda i: (i, 0, 0))
    views, specs = [], []
    for _, dil in BRANCHES:
        groups = PAST_LEN // dil
        blk = groups // BAND - 1
        if dil == 1:
            views.append(lambda c: c)
            specs.append(pl.BlockSpec((None, BAND, N_HEADS, HEAD_DIM),
                                      lambda i, blk=blk: (i, blk, 0, 0)))
        else:
            views.append(lambda c, groups=groups, dil=dil:
                         c.reshape(b, groups, dil, N_HEADS, HEAD_DIM))
            specs.append(pl.BlockSpec((None, BAND, None, N_HEADS, HEAD_DIM),
                                      lambda i, blk=blk: (i, blk, 0, 0, 0)))
    ks = [f(cache_k) for f in views]
    vs = [f(cache_v) for f in views]
    return pl.pallas_call(
        _attn_sample_kernel,
        grid=(b,),
        in_specs=[row, row, row] + specs + specs,
        out_specs=row,
        out_shape=jax.ShapeDtypeStruct((b, N_HEADS, HEAD_DIM), F32),
        compiler_params=_cparams(1),
        name="attn_sample",
    )(q, kn, vn, *ks, *vs)


def _split_bf16(x):
    hi = x.astype(BF16)
    lo = (x - hi.astype(F32)).astype(BF16)
    return hi, lo


def _route_topk(logits_t, b_col, tm):
    scores = _sigmoid(logits_t)
    biased = scores + b_col
    e_iota = lax.broadcasted_iota(I32, (N_EXPERTS, tm), 0)
    grp = biased.reshape(N_GROUPS, GROUP_SIZE, tm)
    mem = lax.broadcasted_iota(I32, (N_GROUPS, GROUP_SIZE, tm), 1)
    m1 = jnp.max(grp, axis=1, keepdims=True)
    first = jnp.min(jnp.where(grp == m1, mem, GROUP_SIZE), axis=1, keepdims=True)
    m2 = jnp.max(jnp.where(mem == first, -jnp.inf, grp), axis=1, keepdims=True)
    gscore = jnp.broadcast_to(m1 + m2, (N_GROUPS, GROUP_SIZE, tm)).reshape(N_EXPERTS, tm)
    g_iota = e_iota // GROUP_SIZE
    cand = jnp.full((N_EXPERTS, tm), -jnp.inf, F32)
    for _ in range(TOPK_GROUPS):
        gm = jnp.max(gscore, axis=0, keepdims=True)
        gi = jnp.min(jnp.where(gscore == gm, g_iota, N_GROUPS), axis=0, keepdims=True)
        hit = g_iota == gi
        cand = jnp.where(hit, biased, cand)
        gscore = jnp.where(hit, -jnp.inf, gscore)
    idx_rows, w_rows = [], []
    for _ in range(TOP_K):
        mx = jnp.max(cand, axis=0, keepdims=True)
        ei = jnp.min(jnp.where(cand == mx, e_iota, N_EXPERTS), axis=0, keepdims=True)
        hit = e_iota == ei
        idx_rows.append(ei)
        w_rows.append(jnp.sum(jnp.where(hit, scores, 0.0), axis=0, keepdims=True))
        cand = jnp.where(hit, -jnp.inf, cand)
    idx = jnp.concatenate(idx_rows, axis=0)
    w = jnp.concatenate(w_rows, axis=0)
    w = w / jnp.sum(w, axis=0, keepdims=True) * ROUTED_SCALE
    return idx, w


def _pack_bf16_pairs(u):
    half = u.shape[1] // 2
    hi = pltpu.bitcast(u[:, :half].astype(BF16).astype(F32), U32)
    lo = pltpu.bitcast(u[:, half:].astype(BF16).astype(F32), U32)
    return hi | (lo >> 16)


def _unpack_bf16_pairs(w):
    hi = pltpu.bitcast(w & jnp.uint32(0xFFFF0000), F32)
    lo = pltpu.bitcast(w << 16, F32)
    return jnp.concatenate([hi, lo], axis=-1).astype(BF16)


def _mix_ffn_kernel(x_ref, mr_ref, oa_ref, ga_ref, wo_ref, gf_ref, wsgu_ref, wsd_ref, wrh_ref,
                    wrl_ref, br_ref, hb_ref, u2p_ref, idx_ref, wt_ref, *, tm):
    att = _rms(oa_ref[...], ga_ref[...]).astype(BF16)
    mix = jnp.concatenate([mr_ref[...], att], axis=-1)
    h1 = x_ref[...] + jnp.dot(mix, wo_ref[...], preferred_element_type=F32)
    u2 = _rms(h1, gf_ref[...])
    u2b = u2.astype(BF16)
    gu = jnp.dot(u2b, wsgu_ref[...], preferred_element_type=F32)
    g, up = gu[:, :D_SHARED], gu[:, D_SHARED:]
    hid = (g * _sigmoid(g) * up).astype(BF16)
    shared = jnp.dot(hid, wsd_ref[...], preferred_element_type=F32)
    hb_ref[...] = h1 + shared

    u_hi, u_lo = u2b, (u2 - u2b.astype(F32)).astype(BF16)
    nt = (((1,), (1,)), ((), ()))
    logits_t = (lax.dot_general(wrh_ref[...], u_hi, nt, preferred_element_type=F32)
                + lax.dot_general(wrh_ref[...], u_lo, nt, preferred_element_type=F32)
                + lax.dot_general(wrl_ref[...], u_hi, nt, preferred_element_type=F32))
    idx, w = _route_topk(logits_t, br_ref[...], tm)
    idx_ref[...] = idx
    wt_ref[...] = w

    words = _pack_bf16_pairs(u2)
    chunks = words.shape[1] // LANES
    for j in range(chunks):
        u2p_ref[pl.ds(j, tm, stride=chunks), :] = words[:, j * LANES:(j + 1) * LANES]


def _mix_ffn(x2d, mix_rnn, o_att, p, tm):
    n = x2d.shape[0]
    row = lambda i: (i, 0)
    col = lambda i: (0, i)
    chunks = D_MODEL // 2 // LANES
    return pl.pallas_call(
        functools.partial(_mix_ffn_kernel, tm=tm),
        grid=(n // tm,),
        in_specs=[pl.BlockSpec((tm, D_MODEL), row), pl.BlockSpec((tm, D_RNN), row),
                  pl.BlockSpec((tm, D_ATT), row), _const_spec((1, D_ATT)),
                  _const_spec((D_MODEL, D_MODEL)), _const_spec((1, D_MODEL)),
                  _const_spec((D_MODEL, 2 * D_SHARED)), _const_spec((D_SHARED, D_MODEL)),
                  _const_spec((N_EXPERTS, D_MODEL)), _const_spec((N_EXPERTS, D_MODEL)),
                  _const_spec((N_EXPERTS, 1))],
        out_specs=[pl.BlockSpec((tm, D_MODEL), row), pl.BlockSpec((tm * chunks, LANES), row),
                   pl.BlockSpec((TOP_K, tm), col), pl.BlockSpec((TOP_K, tm), col)],
        out_shape=[jax.ShapeDtypeStruct((n, D_MODEL), F32),
                   jax.ShapeDtypeStruct((n * chunks, LANES), U32),
                   jax.ShapeDtypeStruct((TOP_K, n), I32),
                   jax.ShapeDtypeStruct((TOP_K, n), F32)],
        compiler_params=_cparams(1),
        name="mix_ffn",
    )(x2d, mix_rnn, o_att, p["g_out_att"], p["w_out"], p["g_ffn"], p["ws_gu"], p["ws_down"],
      p["wr_hi"], p["wr_lo"], p["b_router"])


def _moe_kernel(tile_ref, expert_ref, count_ref, tok_ref,
                wrow_ref, u2p_ref, wg_ref, wu_ref, wd_ref,
                out_ref,
                acc_ref, wgu_ref, wdn_ref, xg_ref, y2_ref, *, tile_tokens):
    w = pl.program_id(0)
    prev = jnp.maximum(w - 1, 0)
    nxt = jnp.minimum(w + 1, pl.num_programs(0) - 1)
    new_tile = jnp.logical_or(w == 0, tile_ref[w] != tile_ref[prev])
    new_expert = jnp.logical_or(w == 0, expert_ref[w] != expert_ref[prev])
    last_of_tile = jnp.logical_or(w == pl.num_programs(0) - 1, tile_ref[nxt] != tile_ref[w])
    x_chunks = D_MODEL // 2 // LANES
    y_chunks = D_MODEL // LANES

    @pl.when(new_tile)
    def _():
        acc_ref[...] = jnp.zeros_like(acc_ref)

    @pl.when(new_expert)
    def _():
        wgu_ref[:, :D_EXPERT] = wg_ref[...].astype(BF16)
        wgu_ref[:, D_EXPERT:] = wu_ref[...].astype(BF16)
        wdn_ref[...] = wd_ref[...].astype(BF16)

    @pl.when(w < count_ref[0])
    def _():
        def token(mi):
            word = tok_ref[w * (MOE_ROWS // 2) + mi // 2]
            return (word >> 16) if mi % 2 else (word & 0xFFFF)

        for mi in range(MOE_ROWS):
            t = jnp.minimum(token(mi), tile_tokens - 1)
            slab = u2p_ref[pl.ds(pl.multiple_of(t * x_chunks, x_chunks), x_chunks), :]
            xg_ref[pl.ds(mi, x_chunks, stride=MOE_STRIDE), :] = slab
        words = jnp.concatenate(
            [xg_ref[pl.ds(j * MOE_STRIDE, MOE_ROWS), :] for j in range(x_chunks)], axis=-1)
        xb = _unpack_bf16_pairs(words)
        gu = jnp.dot(xb, wgu_ref[...], preferred_element_type=F32)
        g, up = gu[:, :D_EXPERT], gu[:, D_EXPERT:]
        hid = (g * _sigmoid(g) * up).astype(BF16)
        y = jnp.dot(hid, wdn_ref[...], preferred_element_type=F32)
        eye = (lax.broadcasted_iota(I32, (MOE_ROWS, MOE_ROWS), 0)
               == lax.broadcasted_iota(I32, (MOE_ROWS, MOE_ROWS), 1))
        wcol = jnp.sum(jnp.where(eye, jnp.broadcast_to(wrow_ref[...], (MOE_ROWS, MOE_ROWS)), 0.0),
                       axis=1, keepdims=True)
        y = y * wcol
        for j in range(y_chunks):
            y2_ref[pl.ds(j * MOE_STRIDE, MOE_ROWS), :] = y[:, j * LANES:(j + 1) * LANES]
        group = 4
        for g0 in range(0, MOE_ROWS, group):
            rows, vals = [], []
            for mi in range(g0, g0 + group):
                r = pl.ds(pl.multiple_of(token(mi) * y_chunks, y_chunks), y_chunks)
                rows.append(r)
                vals.append(acc_ref[r, :] + y2_ref[pl.ds(mi, y_chunks, stride=MOE_STRIDE), :])
            for r, val in zip(rows, vals):
                acc_ref[r, :] = val

    @pl.when(last_of_tile)
    def _():
        out_ref[...] = acc_ref[pl.ds(0, tile_tokens * y_chunks), :]


def _moe_dispatch(idx_t, w_t, tile_tokens, n_tiles, n_items_max):
    k, n = idx_t.shape
    m = k * n
    n_keys = n_tiles * N_EXPERTS
    tok = jnp.broadcast_to(jnp.arange(n, dtype=I32)[None, :], (k, n))
    key = (tok // tile_tokens) * N_EXPERTS + idx_t
    key_s, tok_s, w_s = lax.sort((key.reshape(m), tok.reshape(m), w_t.reshape(m)), num_keys=1,
                                 is_stable=True)
    edges = jnp.searchsorted(key_s, jnp.arange(n_keys + 1, dtype=I32), side="left").astype(I32)
    start, counts = edges[:-1], edges[1:] - edges[:-1]
    n_blk = (counts + MOE_ROWS - 1) // MOE_ROWS
    blk_end = jnp.cumsum(n_blk).astype(I32)
    blk_start = blk_end - n_blk
    n_items = blk_end[-1]
    item = jnp.arange(n_items_max, dtype=I32)
    item_key = jnp.searchsorted(blk_end, jnp.minimum(item, n_items - 1), side="right").astype(I32)
    item_key = jnp.minimum(item_key, n_keys - 1)
    slot = jnp.arange(n_items_max * MOE_ROWS, dtype=I32)
    s_item = slot // MOE_ROWS
    s_key = item_key[s_item]
    j = (s_item - blk_start[s_key]) * MOE_ROWS + slot % MOE_ROWS
    valid = (s_item < n_items) & (j < counts[s_key])
    src = jnp.clip(start[s_key] + j, 0, m - 1)
    slot_tok = jnp.where(valid, tok_s[src] % tile_tokens, tile_tokens).astype(I32)
    slot_w = jnp.where(valid, w_s[src], 0.0)
    packed = slot_tok[0::2] | (slot_tok[1::2] << 16)
    return (item_key // N_EXPERTS, item_key % N_EXPERTS, n_items.reshape(1), packed,
            slot_w.reshape(n_items_max, 1, MOE_ROWS))


def _moe(u2p, idx_t, w_t, we_gate, we_up, we_down, tile_tokens):
    n = idx_t.shape[1]
    n_tiles = n // tile_tokens
    n_items_max = (n * TOP_K) // MOE_ROWS + n_tiles * N_EXPERTS
    item_tile, item_expert, n_items, packed, slot_w = _moe_dispatch(
        idx_t, w_t, tile_tokens, n_tiles, n_items_max)
    x_chunks = D_MODEL // 2 // LANES
    y_chunks = D_MODEL // LANES
    by_tile = lambda w, t, e, c, k: (t[w], 0)
    by_expert = lambda w, t, e, c, k: (e[w], 0, 0)
    grid_spec = pltpu.PrefetchScalarGridSpec(
        num_scalar_prefetch=4,
        grid=(n_items_max,),
        in_specs=[pl.BlockSpec((None, 1, MOE_ROWS), lambda w, t, e, c, k: (w, 0, 0)),
                  pl.BlockSpec((tile_tokens * x_chunks, LANES), by_tile),
                  pl.BlockSpec((None, D_MODEL, D_EXPERT), by_expert),
                  pl.BlockSpec((None, D_MODEL, D_EXPERT), by_expert),
                  pl.BlockSpec((None, D_EXPERT, D_MODEL), by_expert)],
        out_specs=pl.BlockSpec((tile_tokens * y_chunks, LANES), by_tile),
        scratch_shapes=[pltpu.VMEM(((tile_tokens + 1) * y_chunks, LANES), F32),
                        pltpu.VMEM((D_MODEL, 2 * D_EXPERT), BF16),
                        pltpu.VMEM((D_EXPERT, D_MODEL), BF16),
                        pltpu.VMEM((x_chunks * MOE_STRIDE, LANES), U32),
                        pltpu.VMEM((y_chunks * MOE_STRIDE, LANES), F32)],
    )
    return pl.pallas_call(
        functools.partial(_moe_kernel, tile_tokens=tile_tokens),
        grid_spec=grid_spec,
        out_shape=jax.ShapeDtypeStruct((n * y_chunks, LANES), F32),
        compiler_params=_cparams(1),
        name="moe_routed",
    )(item_tile, item_expert, n_items, packed, slot_w, u2p, we_gate, we_up, we_down)


def _ple_kernel(hb_ref, rt_ref, p_ref, g_ref, wg_ref, wp_ref, o_ref, *, tm):
    y_chunks = D_MODEL // LANES
    routed = jnp.concatenate([rt_ref[pl.ds(j, tm, stride=y_chunks), :] for j in range(y_chunks)],
                             axis=-1)
    h2 = hb_ref[...] + routed
    u3 = _rms(h2, g_ref[...]).astype(BF16)
    gate = _sigmoid(jnp.dot(u3, wg_ref[...], preferred_element_type=F32))
    proj = jnp.dot(p_ref[...].astype(BF16), wp_ref[...], preferred_element_type=F32)
    o_ref[...] = h2 + gate * proj


def _ple(hbase, routed, pl_in, p, tm, row0):
    n = hbase.shape[0]
    y_chunks = D_MODEL // LANES
    row = lambda i: (i, 0)
    return pl.pallas_call(
        functools.partial(_ple_kernel, tm=tm),
        grid=(n // tm,),
        in_specs=[pl.BlockSpec((tm, D_MODEL), row),
                  pl.BlockSpec((tm * y_chunks, LANES), lambda i: (i + row0, 0)),
                  pl.BlockSpec((tm, PLE_DIM), row), _const_spec((1, D_MODEL)),
                  _const_spec((D_MODEL, D_MODEL)), _const_spec((PLE_DIM, D_MODEL))],
        out_specs=pl.BlockSpec((tm, D_MODEL), row),
        out_shape=jax.ShapeDtypeStruct((n, D_MODEL), F32),
        compiler_params=_cparams(1),
        name="ple",
    )(hbase, routed, pl_in, p["g_ple"], p["w_ple_gate"], p["w_ple_proj"])


def _block_diag(w):
    nb, c, d = w.shape
    eye = jnp.eye(nb, dtype=w.dtype)
    return (eye[:, None, :, None] * w[:, :, None, :]).reshape(nb * c, nb * d)


def _prepare(g_mix, w_in, conv_w, conv_b, rg_wa, rg_ba, rg_wi, rg_bi, rg_lambda, g_q, g_k,
             g_out_rnn, g_out_att, w_out, g_ffn, w_router, b_router, ws_gate, ws_up, ws_down,
             g_ple, w_ple_gate, w_ple_proj):
    vec = lambda a: a.reshape(1, -1).astype(F32)
    wr_t = w_router.T.astype(F32)
    wr_hi, wr_lo = _split_bf16(wr_t)
    return {
        "g_mix": vec(g_mix), "w_in": w_in.astype(BF16),
        "gq_t": vec(jnp.tile(g_q, N_HEADS)), "gk_t": vec(jnp.tile(g_k, N_HEADS)),
        "ones_bd": _block_diag(jnp.ones((N_HEADS, HEAD_DIM, HEAD_DIM), BF16)),
        "rg": {
            "conv_w": conv_w.astype(F32), "conv_b": vec(conv_b),
            "wa": _block_diag(rg_wa).astype(BF16), "ba": vec(rg_ba),
            "wi": _block_diag(rg_wi).astype(BF16), "bi": vec(rg_bi),
            "clam": vec(-RGLRU_C * jax.nn.softplus(-rg_lambda.astype(F32))),
            "g_out": vec(g_out_rnn),
        },
        "g_out_att": vec(g_out_att), "w_out": w_out.astype(BF16), "g_ffn": vec(g_ffn),
        "ws_gu": jnp.concatenate([ws_gate, ws_up], axis=1).astype(BF16),
        "ws_down": ws_down.astype(BF16),
        "wr_hi": wr_hi, "wr_lo": wr_lo, "b_router": b_router.reshape(-1, 1).astype(F32),
        "g_ple": vec(g_ple), "w_ple_gate": w_ple_gate.astype(BF16),
        "w_ple_proj": w_ple_proj.astype(BF16),
    }


def _pick(n, prefs):
    for t in prefs:
        if n % t == 0:
            return t
    raise ValueError(f"no tile for {n}")


def _layer(x_prompt, x_sample, p_prompt, p_sample, cache_k, cache_v, rnn_h0, conv0, p, experts):
    bp, tp, _ = x_prompt.shape
    bs = x_sample.shape[0]
    n_p, n_s = bp * tp, bs
    xp2, xs2 = x_prompt.reshape(n_p, D_MODEL), x_sample.reshape(n_s, D_MODEL)
    tm_p = _pick(tp, (512, 256, 128))
    tm_s = _pick(n_s, (128, 64, 32, 16, 8))

    tabs_p = _rope_tables(jnp.arange(tp, dtype=I32))
    xr, yr, q, k, v = _in_proj(xp2, p["g_mix"], p["w_in"], p["gq_t"], p["gk_t"], p["ones_bd"],
                               tabs_p, tm_p, tp // tm_p)
    seq3 = lambda a: a.reshape(bp, tp, -1)
    mix_rnn_p, h_last_p, tail_p = _rglru_prompt(seq3(xr), seq3(yr), p["rg"], tm_p)
    o_att_p = _attn_prompt(seq3(q), seq3(k), seq3(v))
    keep = min(BRANCHES[-1][0], tp)
    win_k_p = seq3(k)[:, -keep:].reshape(bp, keep, N_HEADS, HEAD_DIM)
    win_v_p = seq3(v)[:, -keep:].reshape(bp, keep, N_HEADS, HEAD_DIM)

    tabs_s = _rope_tables(jnp.full((tm_s,), PAST_LEN, I32))
    xr_s, yr_s, q_s, k_s, v_s = _in_proj(xs2, p["g_mix"], p["w_in"], p["gq_t"], p["gk_t"],
                                         p["ones_bd"], tabs_s, tm_s, 1)
    mix_rnn_s, h_s, conv_s = _rglru_sample(xr_s, yr_s, jnp.swapaxes(conv0, 0, 1), rnn_h0, p["rg"])
    heads = lambda a: a.reshape(bs, N_HEADS, HEAD_DIM)
    o_att_s = _attn_sample(heads(q_s), heads(k_s), heads(v_s), cache_k, cache_v)
    o_att_s = o_att_s.reshape(bs, D_ATT)

    hb_p, u2p_p, idx_p, wt_p = _mix_ffn(xp2, mix_rnn_p.reshape(n_p, D_RNN),
                                        o_att_p.reshape(n_p, D_ATT), p, tm_p)
    hb_s, u2p_s, idx_s, wt_s = _mix_ffn(xs2, mix_rnn_s, o_att_s, p, tm_s)
    n_all = n_p + n_s
    tile_tokens = n_all // 8 if n_all % 64 == 0 else n_all
    routed = _moe(jnp.concatenate([u2p_p, u2p_s], axis=0),
                  jnp.concatenate([idx_p, idx_s], axis=1), jnp.concatenate([wt_p, wt_s], axis=1),
                  experts[0], experts[1], experts[2], tile_tokens)
    y_p = _ple(hb_p, routed, p_prompt.reshape(n_p, PLE_DIM), p, tm_p, 0)
    y_s = _ple(hb_s, routed, p_sample.reshape(n_s, PLE_DIM), p, tm_s, n_p // tm_s)

    return (y_p.reshape(bp, tp, D_MODEL), y_s.reshape(bs, 1, D_MODEL),
            win_k_p, win_v_p,
            k_s.reshape(bs, 1, N_HEADS, HEAD_DIM), v_s.reshape(bs, 1, N_HEADS, HEAD_DIM),
            h_last_p.reshape(bp, D_RNN), h_s,
            tail_p[:, SUBLANES - (CONV_W - 1):], jnp.swapaxes(conv_s, 0, 1))


def kernel(x_prompt, x_sample, p_prompt, p_sample, cache_win_k, cache_win_v, state_rnn_h, state_conv, g_mix, w_in, conv_w, conv_b, rg_wa, rg_ba, rg_wi, rg_bi, rg_lambda, g_q, g_k, g_out_rnn, g_out_att, w_out, g_ffn, w_router, b_router, we_gate, we_up, we_down, ws_gate, ws_up, ws_down, g_ple, w_ple_gate, w_ple_proj):
    depth = g_mix.shape[0]
    assert depth == 1 and x_sample.shape[1] == 1
    i = 0
    p = _prepare(g_mix[i], w_in[i], conv_w[i], conv_b[i], rg_wa[i], rg_ba[i], rg_wi[i], rg_bi[i],
                 rg_lambda[i], g_q[i], g_k[i], g_out_rnn[i], g_out_att[i], w_out[i], g_ffn[i],
                 w_router[i], b_router[i], ws_gate[i], ws_up[i], ws_down[i], g_ple[i],
                 w_ple_gate[i], w_ple_proj[i])
    outs = _layer(x_prompt, x_sample, p_prompt[i], p_sample[i], cache_win_k[i], cache_win_v[i],
                  state_rnn_h[i], state_conv[i], p, (we_gate[i], we_up[i], we_down[i]))
    y_p, y_s = outs[0], outs[1]
    return (y_p, y_s) + tuple(o[None] for o in outs[2:])
```
